```python
import jax, jax.numpy as jnp
from jax import lax
import numpy as np

D_MODEL = 1024
BATCH = 16
SEQ = 2048
DEPTH = 1

CONV_DIM = 1024
CONV_KERNEL = 31
CONV_PAD = (CONV_KERNEL - 1) // 2
N_HEADS = 16
N_KV_HEADS = 4
HEAD_DIM = 64
ATTN_DIM = N_HEADS * HEAD_DIM
KV_DIM = N_KV_HEADS * HEAD_DIM
WINDOW = 128
BLOCK = 128
NEG_INF = -1e30
N_BRANCH = 2
IN_DIM = 2 * CONV_DIM + ATTN_DIM + 2 * KV_DIM + N_BRANCH * D_MODEL
PEER_HEADS = 8
PEER_KEY_DIM = 256
PEER_HALF = PEER_KEY_DIM // 2
N_KEYS = 128
N_EXPERTS = N_KEYS * N_KEYS
PEER_TOPK = 16
TOKEN_CHUNK = 128
EPS = 1e-6

kernel_name = "hybrid_conformer_swa_peer_block"


def rmsnorm(x, g):
    xf = x.astype(jnp.float32)
    y = xf * lax.rsqrt(jnp.mean(xf * xf, axis=-1, keepdims=True) + EPS)
    return (y * g.astype(jnp.float32)).astype(x.dtype)


def layernorm(x, g, b):
    xf = x.astype(jnp.float32)
    mu = jnp.mean(xf, axis=-1, keepdims=True)
    var = jnp.mean(jnp.square(xf - mu), axis=-1, keepdims=True)
    y = (xf - mu) * lax.rsqrt(var + EPS)
    return (y * g.astype(jnp.float32) + b.astype(jnp.float32)).astype(x.dtype)


def conformer_conv(a_in, dw_w, dw_b, ln_g, ln_b, w_pw):
    u = a_in[..., :CONV_DIM] * jax.nn.sigmoid(a_in[..., CONV_DIM:])
    u = lax.conv_general_dilated(
        u, dw_w[:, None, :], window_strides=(1,), padding=[(CONV_PAD, CONV_PAD)],
        dimension_numbers=('NWC', 'WIO', 'NWC'), feature_group_count=CONV_DIM) + dw_b
    u = layernorm(u, ln_g, ln_b)
    u = jax.nn.silu(u)
    return u @ w_pw


def windowed_gqa(q, k, v, sink):
    b, s = q.shape[0], q.shape[1]
    nb = s // BLOCK
    grp = N_HEADS // N_KV_HEADS
    span = BLOCK + 2 * WINDOW
    kp = jnp.pad(k, ((0, 0), (WINDOW, WINDOW), (0, 0), (0, 0)))
    vp = jnp.pad(v, ((0, 0), (WINDOW, WINDOW), (0, 0), (0, 0)))
    qb = q.reshape(b, nb, BLOCK, N_KV_HEADS, grp, HEAD_DIM).transpose(1, 0, 2, 3, 4, 5)
    slopes = jnp.exp2(-8.0 * jnp.arange(1, N_HEADS + 1, dtype=jnp.float32) / N_HEADS)
    slopes = slopes.reshape(N_KV_HEADS, grp)
    r = jnp.arange(BLOCK)[:, None]
    c = jnp.arange(span)[None, :]
    dist = jnp.abs(r + WINDOW - c)
    bias = -slopes[:, :, None, None] * dist.astype(jnp.float32)[None, None]
    in_window = dist <= WINDOW
    sink_f = sink.astype(jnp.float32).reshape(N_KV_HEADS, grp)[None, :, :, None]
    scale = HEAD_DIM ** -0.5

    def block_fn(args):
        i, qi = args
        start = i * BLOCK
        ks = lax.dynamic_slice_in_dim(kp, start, span, axis=1)
        vs = lax.dynamic_slice_in_dim(vp, start, span, axis=1)
        key_pos = start - WINDOW + jnp.arange(span)
        valid = in_window & ((key_pos >= 0) & (key_pos < s))[None, :]
        logits = jnp.einsum('bqkgd,bskd->bkgqs', qi, ks).astype(jnp.float32) * scale + bias
        logits = jnp.where(valid, logits, NEG_INF)
        m = jnp.maximum(logits.max(-1), sink_f)
        p = jnp.exp(logits - m[..., None])
        denom = p.sum(-1) + jnp.exp(sink_f - m)
        o = jnp.einsum('bkgqs,bskd->bqkgd', p, vs.astype(jnp.float32))
        o = o / denom.transpose(0, 3, 1, 2)[..., None]
        return o.astype(q.dtype)

    out = lax.map(block_fn, (jnp.arange(nb), qb))
    return out.transpose(1, 0, 2, 3, 4, 5).reshape(b, s, ATTN_DIM)


def peer_ffn(h, w_query, sub_keys, expert_u, expert_v):
    b, s, d = h.shape
    hc_all = h.reshape((b * s) // TOKEN_CHUNK, TOKEN_CHUNK, d)

    def chunk_fn(hc):
        q = (hc @ w_query).reshape(TOKEN_CHUNK, PEER_HEADS, 2, PEER_HALF)
        sc = jnp.einsum('thpd,pnd->thpn', q, sub_keys).astype(jnp.float32)
        vals, idx = lax.top_k(sc, PEER_TOPK)
        cand = vals[:, :, 0, :, None] + vals[:, :, 1, None, :]
        cand_idx = idx[:, :, 0, :, None] * N_KEYS + idx[:, :, 1, None, :]
        cand = cand.reshape(TOKEN_CHUNK, PEER_HEADS, PEER_TOPK * PEER_TOPK)
        cand_idx = cand_idx.reshape(TOKEN_CHUNK, PEER_HEADS, PEER_TOPK * PEER_TOPK)
        top_s, pos = lax.top_k(cand, PEER_TOPK)
        expert_idx = jnp.take_along_axis(cand_idx, pos, axis=-1)
        gate = jax.nn.softmax(top_s, axis=-1)
        u = expert_u[expert_idx]
        act = jax.nn.gelu(jnp.einsum('thkd,td->thk', u, hc).astype(jnp.float32), approximate=False)
        vv = expert_v[expert_idx]
        return jnp.einsum('thk,thkd->td', (gate * act).astype(hc.dtype), vv)

    return lax.map(chunk_fn, hc_all).reshape(b, s, d)


def setup_inputs(seed: int = 0) -> dict:
    key = jax.random.key(seed)
    ks = jax.random.split(key, 18)
    f32 = jnp.float32
    nrm = lambda k, shape, sc: jax.random.normal(k, shape, f32) * sc
    return {
        "x": nrm(ks[0], (BATCH, SEQ, D_MODEL), 1.0),
        "norm1_g": 1.0 + nrm(ks[1], (DEPTH, D_MODEL), 0.02),
        "w_in": nrm(ks[2], (DEPTH, D_MODEL, IN_DIM), D_MODEL ** -0.5),
        "conv_dw_w": nrm(ks[3], (DEPTH, CONV_KERNEL, CONV_DIM), CONV_KERNEL ** -0.5),
        "conv_dw_b": nrm(ks[4], (DEPTH, CONV_DIM), 0.02),
        "conv_ln_g": 1.0 + nrm(ks[5], (DEPTH, CONV_DIM), 0.02),
        "conv_ln_b": nrm(ks[6], (DEPTH, CONV_DIM), 0.02),
        "conv_w_pw": nrm(ks[7], (DEPTH, CONV_DIM, D_MODEL), CONV_DIM ** -0.5),
        "attn_sink": nrm(ks[8], (DEPTH, N_HEADS), 0.5),
        "attn_w_o": nrm(ks[9], (DEPTH, ATTN_DIM, D_MODEL), ATTN_DIM ** -0.5),
        "w_out": nrm(ks[10], (DEPTH, D_MODEL, D_MODEL), D_MODEL ** -0.5),
        "norm2_g": 1.0 + nrm(ks[11], (DEPTH, D_MODEL), 0.02),
        "peer_w_query": nrm(ks[12], (DEPTH, D_MODEL, PEER_HEADS * PEER_KEY_DIM), D_MODEL ** -0.5),
        "peer_sub_keys": nrm(ks[13], (DEPTH, 2, N_KEYS, PEER_HALF), PEER_HALF ** -0.5),
        "peer_u": nrm(ks[14], (DEPTH, N_EXPERTS, D_MODEL), D_MODEL ** -0.5),
        "peer_v": nrm(ks[15], (DEPTH, N_EXPERTS, D_MODEL), 0.25),
        "final_g": 1.0 + nrm(ks[16], (D_MODEL,), 0.02),
    }


def reference(x, norm1_g, w_in, conv_dw_w, conv_dw_b, conv_ln_g, conv_ln_b, conv_w_pw,
              attn_sink, attn_w_o, w_out, norm2_g, peer_w_query, peer_sub_keys,
              peer_u, peer_v, final_g):
    b, s, d = x.shape
    o_conv = 2 * CONV_DIM
    o_q = o_conv + ATTN_DIM
    o_k = o_q + KV_DIM
    o_v = o_k + KV_DIM
    for l in range(DEPTH):
        h = rmsnorm(x, norm1_g[l])
        z = h @ w_in[l]
        conv_in = z[..., :o_conv]
        q = z[..., o_conv:o_q].reshape(b, s, N_HEADS, HEAD_DIM)
        k = z[..., o_q:o_k].reshape(b, s, N_KV_HEADS, HEAD_DIM)
        v = z[..., o_k:o_v].reshape(b, s, N_KV_HEADS, HEAD_DIM)
        gates = jax.nn.sigmoid(z[..., o_v:].astype(jnp.float32)).astype(x.dtype)
        gates = gates.reshape(b, s, N_BRANCH, d)
        conv_out = conformer_conv(conv_in, conv_dw_w[l], conv_dw_b[l], conv_ln_g[l],
                                  conv_ln_b[l], conv_w_pw[l])
        attn_out = windowed_gqa(q, k, v, attn_sink[l]) @ attn_w_o[l]
        merged = gates[:, :, 0, :] * conv_out + gates[:, :, 1, :] * attn_out
        x = x + merged @ w_out[l]
        h2 = rmsnorm(x, norm2_g[l])
        x = x + peer_ffn(h2, peer_w_query[l], peer_sub_keys[l], peer_u[l], peer_v[l])
    return rmsnorm(x, final_g)
```

```python
import functools

import jax
import jax.numpy as jnp
from jax import lax
from jax.experimental import pallas as pl
from jax.experimental.pallas import tpu as pltpu

D_MODEL = 1024
CONV_DIM = 1024
CONV_KERNEL = 31
CONV_PAD = (CONV_KERNEL - 1) // 2
N_HEADS = 16
N_KV_HEADS = 4
HEAD_DIM = 64
GROUP = N_HEADS // N_KV_HEADS
ATTN_DIM = N_HEADS * HEAD_DIM
KV_DIM = N_KV_HEADS * HEAD_DIM
WINDOW = 128
BLOCK = 128
SPAN = BLOCK + 2 * WINDOW
NEG_INF = -1e30
PEER_HEADS = 8
PEER_HALF = 128
N_KEYS = 128
N_EXPERTS = N_KEYS * N_KEYS
PEER_TOPK = 16
PAIRS = PEER_HEADS * PEER_TOPK
EPS = 1e-6

SUBLANES = 8
LANES = 128
VMEM_LIMIT_BYTES = 56 * 1024 * 1024

TM_IN = 256
TT_CONV = 128
CC_CONV = 256
TM_ROUTE = 256
TB_PEER = 256
TM_FINAL = 1024

HALF_EXPERTS = N_EXPERTS // 2
HI_MASK = 0xFFFF0000


def _cparams(*sem):
    return pltpu.CompilerParams(dimension_semantics=sem, vmem_limit_bytes=VMEM_LIMIT_BYTES)


def _in_proj_kernel(x_ref, g_ref, wa_ref, wb_ref, wq_ref, wkv_ref, wg_ref,
                    u_ref, q_ref, kv_ref, gates_ref):
    xf = x_ref[...]
    h = xf * lax.rsqrt(jnp.mean(xf * xf, axis=-1, keepdims=True) + EPS) * g_ref[...]
    hb = h.astype(jnp.bfloat16)
    za = jnp.dot(hb, wa_ref[...], preferred_element_type=jnp.float32)
    zb = jnp.dot(hb, wb_ref[...], preferred_element_type=jnp.float32)
    u_ref[...] = (za * jax.nn.sigmoid(zb)).astype(u_ref.dtype)
    zq = jnp.dot(hb, wq_ref[...], preferred_element_type=jnp.float32)
    q_ref[...] = (zq * (HEAD_DIM ** -0.5)).astype(q_ref.dtype)
    kv_ref[...] = jnp.dot(hb, wkv_ref[...], preferred_element_type=jnp.float32).astype(kv_ref.dtype)
    zg = jnp.dot(hb, wg_ref[...], preferred_element_type=jnp.float32)
    gates_ref[...] = jax.nn.sigmoid(zg).astype(gates_ref.dtype)


def _in_proj(x2, g, wa, wb, wq, wkv, wg):
    t = x2.shape[0]
    row = lambda n: pl.BlockSpec((TM_IN, n), lambda i: (i, 0))
    full = lambda a: pl.BlockSpec(a.shape, lambda i: (0, 0))
    bf = jnp.bfloat16
    return pl.pallas_call(
        _in_proj_kernel,
        grid=(t // TM_IN,),
        in_specs=[row(D_MODEL), full(g), full(wa), full(wb), full(wq), full(wkv), full(wg)],
        out_specs=[row(CONV_DIM), row(ATTN_DIM), row(2 * KV_DIM), row(2 * D_MODEL)],
        out_shape=[jax.ShapeDtypeStruct((t, CONV_DIM), bf), jax.ShapeDtypeStruct((t, ATTN_DIM), bf),
                   jax.ShapeDtypeStruct((t, 2 * KV_DIM), bf), jax.ShapeDtypeStruct((t, 2 * D_MODEL), bf)],
        compiler_params=_cparams("parallel"),
        name="in_proj",
    )(x2, g, wa, wb, wq, wkv, wg)


def _conv_kernel(u_ref, w_ref, b_ref, lg_ref, lb_ref, o_ref, upad_ref, y_ref):
    s = u_ref.shape[1]
    halo = 2 * SUBLANES
    win = TT_CONV + 2 * halo
    zeros = jnp.zeros((halo, CONV_DIM), jnp.float32)
    upad_ref[0:halo, :] = zeros
    upad_ref[halo + s:halo + s + halo, :] = zeros
    upad_ref[halo:halo + s, :] = u_ref[0].astype(jnp.float32)

    def tile(ti, carry):
        t0 = pl.multiple_of(ti * TT_CONV, TT_CONV)
        for c0 in range(0, CONV_DIM, CC_CONV):
            w = upad_ref[pl.ds(t0, win), c0:c0 + CC_CONV]
            acc = jnp.zeros((TT_CONV, CC_CONV), jnp.float32)
            for r in range(SUBLANES):
                wr = w if r == 0 else pltpu.roll(w, win - r, 0)
                for q in range(4):
                    j = q * SUBLANES + r - (halo - CONV_PAD)
                    if 0 <= j < CONV_KERNEL:
                        acc = acc + wr[q * SUBLANES:q * SUBLANES + TT_CONV, :] * w_ref[j:j + 1, c0:c0 + CC_CONV]
            y_ref[:, c0:c0 + CC_CONV] = acc + b_ref[:, c0:c0 + CC_CONV]
        y = y_ref[...]
        mu = jnp.mean(y, axis=-1, keepdims=True)
        yc = y - mu
        var = jnp.mean(yc * yc, axis=-1, keepdims=True)
        z = yc * lax.rsqrt(var + EPS) * lg_ref[...] + lb_ref[...]
        o_ref[0, pl.ds(t0, TT_CONV), :] = (z * jax.nn.sigmoid(z)).astype(o_ref.dtype)
        return carry

    lax.fori_loop(0, s // TT_CONV, tile, 0)


def _conv(u3, dw_w, dw_b, ln_g, ln_b):
    b, s, _ = u3.shape
    full = lambda a: pl.BlockSpec(a.shape, lambda i: (0, 0))
    return pl.pallas_call(
        _conv_kernel,
        grid=(b,),
        in_specs=[pl.BlockSpec((1, s, CONV_DIM), lambda i: (i, 0, 0)),
                  full(dw_w), full(dw_b), full(ln_g), full(ln_b)],
        out_specs=pl.BlockSpec((1, s, CONV_DIM), lambda i: (i, 0, 0)),
        out_shape=jax.ShapeDtypeStruct((b, s, CONV_DIM), jnp.bfloat16),
        scratch_shapes=[pltpu.VMEM((s + 4 * SUBLANES, CONV_DIM), jnp.float32),
                        pltpu.VMEM((TT_CONV, CONV_DIM), jnp.float32)],
        compiler_params=_cparams("parallel"),
        name="conv",
    )(u3, dw_w, dw_b, ln_g, ln_b)


def _attn_kernel(sink_ref, q_ref, kvp_ref, kvc_ref, kvn_ref, o_ref, *, seq):
    i = pl.program_id(1)
    bf = jnp.bfloat16
    kv = jnp.concatenate([kvp_ref[0], kvc_ref[0], kvn_ref[0]], axis=0)
    k_blk = kv[:, :KV_DIM]
    v_blk = kv[:, KV_DIM:]
    r = lax.broadcasted_iota(jnp.int32, (BLOCK, SPAN), 0)
    c = lax.broadcasted_iota(jnp.int32, (BLOCK, SPAN), 1)
    dist = jnp.abs(r + WINDOW - c)
    key_pos = i * BLOCK - WINDOW + c
    valid = (dist <= WINDOW) & (key_pos >= 0) & (key_pos < seq)
    distf = dist.astype(jnp.float32)
    width = GROUP * HEAD_DIM
    lane_head = lax.broadcasted_iota(jnp.int32, (BLOCK, width), 1) // HEAD_DIM
    pj = lax.broadcasted_iota(jnp.int32, (KV_DIM, width), 0)
    pn = lax.broadcasted_iota(jnp.int32, (KV_DIM, width), 1)
    for g in range(N_KV_HEADS):
        rep = (pj == g * HEAD_DIM + pn % HEAD_DIM).astype(bf)
        kt = jnp.dot(k_blk, rep, preferred_element_type=jnp.float32).astype(bf)
        vt = jnp.dot(v_blk, rep, preferred_element_type=jnp.float32).astype(bf)
        qg = q_ref[0, :, g * width:(g + 1) * width]
        qs = jnp.concatenate([jnp.where(lane_head == hh, qg, jnp.zeros_like(qg)) for hh in range(GROUP)], axis=0)
        logits = lax.dot_general(qs, kt, (((1,), (1,)), ((), ())), preferred_element_type=jnp.float32)
        ps, inv = [], []
        for hh in range(GROUP):
            head = g * GROUP + hh
            slope = 2.0 ** (-8.0 * (head + 1) / N_HEADS)
            sink = sink_ref[head]
            lg = logits[hh * BLOCK:(hh + 1) * BLOCK, :] - slope * distf
            lg = jnp.where(valid, lg, NEG_INF)
            m = jnp.maximum(jnp.max(lg, axis=-1, keepdims=True), sink)
            p = jnp.exp(lg - m)
            denom = jnp.sum(p, axis=-1, keepdims=True) + jnp.exp(sink - m)
            ps.append(p.astype(bf))
            inv.append(1.0 / denom)
        o_all = jnp.dot(jnp.concatenate(ps, axis=0), vt, preferred_element_type=jnp.float32)
        og = jnp.zeros((BLOCK, width), jnp.float32)
        for hh in range(GROUP):
            og = og + jnp.where(lane_head == hh, o_all[hh * BLOCK:(hh + 1) * BLOCK, :] * inv[hh], 0.0)
        o_ref[0, :, g * width:(g + 1) * width] = og.astype(o_ref.dtype)


def _attn(q3, kv3, sink):
    b, s, _ = q3.shape
    nb = s // BLOCK
    kv_spec = lambda f: pl.BlockSpec((1, BLOCK, 2 * KV_DIM), f)
    return pl.pallas_call(
        functools.partial(_attn_kernel, seq=s),
        grid=(b, nb),
        in_specs=[pl.BlockSpec(memory_space=pltpu.SMEM),
                  pl.BlockSpec((1, BLOCK, ATTN_DIM), lambda bi, i: (bi, i, 0)),
                  kv_spec(lambda bi, i: (bi, jnp.maximum(i - 1, 0), 0)),
                  kv_spec(lambda bi, i: (bi, i, 0)),
                  kv_spec(lambda bi, i: (bi, jnp.minimum(i + 1, nb - 1), 0))],
        out_specs=pl.BlockSpec((1, BLOCK, ATTN_DIM), lambda bi, i: (bi, i, 0)),
        out_shape=jax.ShapeDtypeStruct((b, s, ATTN_DIM), jnp.bfloat16),
        compiler_params=_cparams("parallel", "parallel"),
        name="attn",
    )(sink, q3, kv3, kv3, kv3)


def _top16(x, key, big):
    vals, keys = [], []
    for _ in range(PEER_TOPK):
        m = jnp.max(x, axis=0, keepdims=True)
        kmin = jnp.min(jnp.where(x == m, key, big), axis=0, keepdims=True)
        vals.append(m)
        keys.append(kmin)
        x = jnp.where(key == kmin, -jnp.inf, x)
    return vals, keys


def _route_kernel(c_ref, a_ref, gates_ref, x_ref, wpw_ref, wo_ref, wout_ref, g2_ref, wq_ref, keys_ref,
                  x1_ref, h2_ref, idx_ref, gate_ref, q2_ref, vals_ref, kidx_ref):
    bf = jnp.bfloat16
    f32 = jnp.float32
    tm = x_ref.shape[0]
    conv_out = jnp.dot(c_ref[...], wpw_ref[...], preferred_element_type=f32)
    attn_out = jnp.dot(a_ref[...], wo_ref[...], preferred_element_type=f32)
    merged = (gates_ref[:, :D_MODEL].astype(f32) * conv_out
              + gates_ref[:, D_MODEL:].astype(f32) * attn_out)
    x1 = x_ref[...] + jnp.dot(merged.astype(bf), wout_ref[...], preferred_element_type=f32)
    x1_ref[...] = x1
    h2 = x1 * lax.rsqrt(jnp.mean(x1 * x1, axis=-1, keepdims=True) + EPS) * g2_ref[...]
    h2_ref[...] = h2
    q2 = jnp.dot(h2.astype(bf), wq_ref[...], preferred_element_type=f32).astype(bf)
    n_chunks = 2 * PEER_HEADS
    for cix in range(n_chunks):
        q2_ref[cix] = q2[:, cix * PEER_HALF:(cix + 1) * PEER_HALF]

    key_iota = lax.broadcasted_iota(jnp.int32, (N_KEYS, tm), 0).astype(f32)

    def chunk(cix, carry):
        sc = lax.dot_general(keys_ref[cix % 2], q2_ref[cix], (((1,), (1,)), ((), ())),
                             preferred_element_type=f32)
        vals, keys = _top16(sc, key_iota, float(N_KEYS))
        vals_ref[cix] = jnp.concatenate(vals, axis=0)
        kidx_ref[cix] = jnp.concatenate(keys, axis=0)
        return carry

    lax.fori_loop(0, n_chunks, chunk, 0)

    sub = lax.broadcasted_iota(jnp.int32, (SUBLANES, tm), 0).astype(f32)

    def head(hd, carry):
        v0, v1 = vals_ref[2 * hd], vals_ref[2 * hd + 1]
        i0, i1 = kidx_ref[2 * hd], kidx_ref[2 * hd + 1]
        cand, cexp, cpos = [], [], []

        def add(va, vb, ia, ib, pos):
            cand.append(va + vb)
            cexp.append(ia * float(N_KEYS) + ib)
            cpos.append(pos)

        add(v0[0:1], v1[0:8], i0[0:1], i1[0:8], sub)
        add(v0[0:1], v1[8:16], i0[0:1], i1[8:16], sub + 8.0)
        for a in range(1, SUBLANES):
            add(v0[a:a + 1], v1[0:8], i0[a:a + 1], i1[0:8], sub + float(a * PEER_TOPK))
        add(v0[8:16], v1[0:1], i0[8:16], i1[0:1], (sub + 8.0) * float(PEER_TOPK))
        cand = jnp.concatenate(cand, axis=0)
        cexp = jnp.concatenate(cexp, axis=0)
        cpos = jnp.concatenate(cpos, axis=0)
        tops, sel = [], []
        for _ in range(PEER_TOPK):
            m = jnp.max(cand, axis=0, keepdims=True)
            pmin = jnp.min(jnp.where(cand == m, cpos, 1e9), axis=0, keepdims=True)
            hit = cpos == pmin
            tops.append(m)
            sel.append(jnp.max(jnp.where(hit, cexp, -1.0), axis=0, keepdims=True))
            cand = jnp.where(hit, -jnp.inf, cand)
        top_s = jnp.concatenate(tops, axis=0)
        e = jnp.exp(top_s - jnp.max(top_s, axis=0, keepdims=True))
        gate_ref[hd] = e / jnp.sum(e, axis=0, keepdims=True)
        idx_ref[hd] = jnp.concatenate(sel, axis=0).astype(jnp.int32)
        return carry

    lax.fori_loop(0, PEER_HEADS, head, 0)


def _route(c2, a2, gates, x2, wpw, wo, wout, g2, wq, keys):
    t = x2.shape[0]
    tm = TM_ROUTE
    row = lambda n: pl.BlockSpec((tm, n), lambda i: (i, 0))
    full = lambda a: pl.BlockSpec(a.shape, lambda i: (0,) * a.ndim)
    sel = pl.BlockSpec((PEER_HEADS, PEER_TOPK, tm), lambda i: (0, 0, i))
    return pl.pallas_call(
        _route_kernel,
        grid=(t // tm,),
        in_specs=[row(CONV_DIM), row(ATTN_DIM), row(2 * D_MODEL), row(D_MODEL),
                  full(wpw), full(wo), full(wout), full(g2), full(wq), full(keys)],
        out_specs=[row(D_MODEL), row(D_MODEL), sel, sel],
        out_shape=[jax.ShapeDtypeStruct((t, D_MODEL), jnp.float32),
                   jax.ShapeDtypeStruct((t, D_MODEL), jnp.float32),
                   jax.ShapeDtypeStruct((PEER_HEADS, PEER_TOPK, t), jnp.int32),
                   jax.ShapeDtypeStruct((PEER_HEADS, PEER_TOPK, t), jnp.float32)],
        scratch_shapes=[pltpu.VMEM((2 * PEER_HEADS, tm, PEER_HALF), jnp.bfloat16),
                        pltpu.VMEM((2 * PEER_HEADS, PEER_TOPK, tm), jnp.float32),
                        pltpu.VMEM((2 * PEER_HEADS, PEER_TOPK, tm), jnp.float32)],
        compiler_params=_cparams("parallel"),
        name="route",
    )(c2, a2, gates, x2, wpw, wo, wout, g2, wq, keys)


def _pack_table(tbl):
    bits = lax.bitcast_convert_type(tbl.astype(jnp.bfloat16), jnp.uint16).astype(jnp.uint32)
    packed = bits[:HALF_EXPERTS] | (bits[HALF_EXPERTS:] << 16)
    return packed.reshape(HALF_EXPERTS, SUBLANES, LANES)


def _expert_row(tbl_ref, e):
    word = tbl_ref[e & (HALF_EXPERTS - 1)]
    shift = (16 - ((e >> 13) << 4)).astype(jnp.uint32)
    word = lax.shift_left(word, jnp.full(word.shape, shift, jnp.uint32)) & jnp.uint32(HI_MASK)
    return lax.bitcast_convert_type(word, jnp.float32)


def _fold(xs, shift, size, axis, index):
    n = len(xs) // 2
    low = (index & shift) == 0
    out = []
    for i in range(n):
        a, b = xs[i], xs[i + n]
        out.append(jnp.where(low, a, pltpu.roll(b, shift, axis)) + jnp.where(low, pltpu.roll(a, size - shift, axis), b))
    return out


def _peer_u_kernel(idx_ref, h_ref, gate_ref, u_ref, coef_ref, r_ref):
    tb = h_ref.shape[0]
    sub = lax.broadcasted_iota(jnp.int32, (SUBLANES, LANES), 0)
    lane = lax.broadcasted_iota(jnp.int32, (SUBLANES, LANES), 1)

    def group(tg, carry):
        def token(j, carry2):
            t = tg * SUBLANES + j
            hv = h_ref[t]
            qs = []
            for g in range(PAIRS // SUBLANES):
                ps = [_expert_row(u_ref, idx_ref[t, g * SUBLANES + k]) * hv for k in range(SUBLANES)]
                for shift in (4, 2, 1):
                    ps = _fold(ps, shift, SUBLANES, 0, sub)
                qs.append(ps[0])
            for shift in (64, 32, 16, 8):
                qs = _fold(qs, shift, LANES, 1, lane)
            r_ref[j] = qs[0]
            return carry2

        lax.fori_loop(0, SUBLANES, token, 0)
        rs = [r_ref[j] for j in range(SUBLANES)]
        for shift in (4, 2, 1):
            rs = _fold(rs, shift, LANES, 1, lane)
        act = rs[0]
        gelu = 0.5 * act * (1.0 + lax.erf(act * (2.0 ** -0.5)))
        coef_ref[tg] = gate_ref[tg] * gelu
        return carry

    lax.fori_loop(0, tb // SUBLANES, group, 0)


def _peer_u(idx, h3, gate_d, u_packed):
    t = h3.shape[0]
    tb = TB_PEER
    dense = pl.BlockSpec((tb // SUBLANES, SUBLANES, LANES), lambda i: (i, 0, 0))
    return pl.pallas_call(
        _peer_u_kernel,
        grid=(t // tb,),
        in_specs=[pl.BlockSpec((tb, PAIRS), lambda i: (i, 0), memory_space=pltpu.SMEM),
                  pl.BlockSpec((tb, SUBLANES, LANES), lambda i: (i, 0, 0)),
                  dense,
                  pl.BlockSpec(memory_space=pltpu.VMEM)],
        out_specs=dense,
        out_shape=jax.ShapeDtypeStruct((t // SUBLANES, SUBLANES, LANES), jnp.float32),
        scratch_shapes=[pltpu.VMEM((SUBLANES, SUBLANES, LANES), jnp.float32)],
        compiler_params=_cparams("arbitrary"),
        name="peer_u",
    )(idx, h3, gate_d, u_packed)


def _peer_v_kernel(idx_ref, coef_ref, x1_ref, v_ref, o_ref):
    tb = x1_ref.shape[0]
    n_acc = 4

    def group(tg, carry):
        def token(j, carry2):
            t = tg * SUBLANES + j
            accs = [jnp.zeros((SUBLANES, LANES), jnp.float32) for _ in range(n_acc)]
            for p in range(PAIRS):
                g, k = divmod(p, SUBLANES)
                coef = coef_ref[tg * SUBLANES + k, g * SUBLANES + j]
                accs[p % n_acc] = accs[p % n_acc] + coef * _expert_row(v_ref, idx_ref[t, p])
            o_ref[t] = x1_ref[t] + ((accs[0] + accs[1]) + (accs[2] + accs[3]))
            return carry2

        lax.fori_loop(0, SUBLANES, token, 0)
        return carry

    lax.fori_loop(0, tb // SUBLANES, group, 0)


def _peer_v(idx, coef2, x13, v_packed):
    t = x13.shape[0]
    tb = TB_PEER
    smem = pl.BlockSpec((tb, PAIRS), lambda i: (i, 0), memory_space=pltpu.SMEM)
    tok = pl.BlockSpec((tb, SUBLANES, LANES), lambda i: (i, 0, 0))
    return pl.pallas_call(
        _peer_v_kernel,
        grid=(t // tb,),
        in_specs=[smem, smem, tok, pl.BlockSpec(memory_space=pltpu.VMEM)],
        out_specs=tok,
        out_shape=jax.ShapeDtypeStruct((t, SUBLANES, LANES), jnp.float32),
        compiler_params=_cparams("arbitrary"),
        name="peer_v",
    )(idx, coef2, x13, v_packed)


def _final_kernel(x_ref, g_ref, o_ref):
    xf = x_ref[...]
    o_ref[...] = xf * lax.rsqrt(jnp.mean(xf * xf, axis=-1, keepdims=True) + EPS) * g_ref[...]


def _final(x2, g):
    t = x2.shape[0]
    row = pl.BlockSpec((TM_FINAL, D_MODEL), lambda i: (i, 0))
    return pl.pallas_call(
        _final_kernel,
        grid=(t // TM_FINAL,),
        in_specs=[row, pl.BlockSpec(g.shape, lambda i: (0, 0))],
        out_specs=row,
        out_shape=jax.ShapeDtypeStruct((t, D_MODEL), jnp.float32),
        compiler_params=_cparams("parallel"),
        name="final_norm",
    )(x2, g)


def _layer(x, norm1_g, w_in, conv_dw_w, conv_dw_b, conv_ln_g, conv_ln_b, conv_w_pw,
           attn_sink, attn_w_o, w_out, norm2_g, peer_w_query, peer_sub_keys, peer_u, peer_v):
    b, s, d = x.shape
    t = b * s
    bf = jnp.bfloat16
    row2 = lambda v: v.reshape(1, -1).astype(jnp.float32)
    o_conv = 2 * CONV_DIM
    o_q = o_conv + ATTN_DIM
    o_v = o_q + 2 * KV_DIM
    w = w_in.astype(bf)
    x2 = x.reshape(t, d)
    u, q, kv, gates = _in_proj(x2, row2(norm1_g), w[:, :CONV_DIM], w[:, CONV_DIM:o_conv],
                               w[:, o_conv:o_q], w[:, o_q:o_v], w[:, o_v:])
    c = _conv(u.reshape(b, s, CONV_DIM), conv_dw_w.astype(jnp.float32), row2(conv_dw_b),
              row2(conv_ln_g), row2(conv_ln_b))
    a = _attn(q.reshape(b, s, ATTN_DIM), kv.reshape(b, s, 2 * KV_DIM), attn_sink.astype(jnp.float32))
    x1, h2, idx_t, gate_t = _route(c.reshape(t, CONV_DIM), a.reshape(t, ATTN_DIM), gates, x2,
                                   conv_w_pw.astype(bf), attn_w_o.astype(bf), w_out.astype(bf),
                                   row2(norm2_g), peer_w_query.astype(bf), peer_sub_keys.astype(bf))
    idx = idx_t.reshape(PAIRS, t).T
    gate_d = (gate_t.reshape(PAIRS // SUBLANES, SUBLANES, t // SUBLANES, SUBLANES)
              .transpose(2, 1, 0, 3).reshape(t // SUBLANES, SUBLANES, LANES))
    coef = _peer_u(idx, h2.reshape(t, SUBLANES, LANES), gate_d, _pack_table(peer_u))
    x_out = _peer_v(idx, coef.reshape(t, PAIRS), x1.reshape(t, SUBLANES, LANES), _pack_table(peer_v))
    return x_out.reshape(b, s, d)


def kernel(x, norm1_g, w_in, conv_dw_w, conv_dw_b, conv_ln_g, conv_ln_b, conv_w_pw, attn_sink, attn_w_o, w_out, norm2_g, peer_w_query, peer_sub_keys, peer_u, peer_v, final_g):
    depth = norm1_g.shape[0]
    for l in range(depth):
        x = _layer(x, norm1_g[l], w_in[l], conv_dw_w[l], conv_dw_b[l], conv_ln_g[l], conv_ln_b[l],
                   conv_w_pw[l], attn_sink[l], attn_w_o[l], w_out[l], norm2_g[l], peer_w_query[l],
                   peer_sub_keys[l], peer_u[l], peer_v[l])
    b, s, d = x.shape
    y = _final(x.reshape(b * s, d), final_g.reshape(1, -1).astype(jnp.float32))
    return y.reshape(b, s, d)
```
